```python
import jax, jax.numpy as jnp
from jax import lax
import numpy as np

D_MODEL = 1024
BATCH = 32
SEQ = 2048
DEPTH = 4
DEC_BATCH = 16
DEC_SEQ = 32
PAST_LEN = 4096

CHUNK = 64
POOL_WINDOWS = (2, 4, 8, 16)
POOL_GROUPS = len(POOL_WINDOWS)
POOL_WIDTH = D_MODEL // 2
POOL_GD = POOL_WIDTH // POOL_GROUPS
POOL_BUF = max(POOL_WINDOWS) - 1
RET_HEADS = 4
RET_DK = D_MODEL // 8
RET_DV = D_MODEL // 8
RET_WIDTH = RET_HEADS * RET_DV
N_BRANCH = 2
IN_WIDTH = POOL_WIDTH + 4 * RET_WIDTH + N_BRANCH * D_MODEL
D_FF = ((8 * D_MODEL // 3 + 127) // 128) * 128
N_MOD = 9
ROPE_BASE = 10000.0
EPS = 1e-6

kernel_name = "hybrid_pool_retention_streaming_step"

F32 = jnp.float32


def rmsnorm(x, g):
    xf = x.astype(F32)
    r = lax.rsqrt(jnp.mean(xf * xf, axis=-1, keepdims=True) + EPS)
    return (xf * r).astype(x.dtype) * g


def swiglu(h, w_up, w_down):
    a, b = jnp.split(h @ w_up, 2, axis=-1)
    return (jax.nn.silu(a) * b) @ w_down


def rope(x, pos):
    half = x.shape[-1] // 2
    inv = ROPE_BASE ** (-jnp.arange(half, dtype=F32) / half)
    ang = pos.astype(F32)[:, None] * inv[None, :]
    cos = jnp.cos(ang)[None, :, None, :]
    sin = jnp.sin(ang)[None, :, None, :]
    x1 = x[..., :half].astype(F32)
    x2 = x[..., half:].astype(F32)
    return jnp.concatenate([x1 * cos - x2 * sin, x2 * cos + x1 * sin], axis=-1).astype(x.dtype)


def log_gammas():
    return jnp.log1p(-jnp.exp2(-5.0 - jnp.arange(RET_HEADS, dtype=F32)))


def pool_mix(u, buf, start, pool_map, pool_scale):
    B, T, P = u.shape
    L = POOL_BUF
    full = jnp.concatenate([buf.astype(u.dtype), u], axis=1)
    cs0 = jnp.pad(jnp.cumsum(full.astype(F32), axis=1), ((0, 0), (1, 0), (0, 0)))
    pos = start + jnp.arange(T)
    ys = []
    for gi, w in enumerate(POOL_WINDOWS):
        sl = slice(gi * POOL_GD, (gi + 1) * POOL_GD)
        s = cs0[:, L + 1:L + 1 + T, sl] - cs0[:, L + 1 - w:L + 1 - w + T, sl]
        cnt = jnp.minimum(w, pos + 1).astype(F32)
        ys.append((s / cnt[None, :, None]).astype(u.dtype) - u[..., sl])
    y = jnp.stack(ys, axis=2)
    y = jnp.einsum("btgc,gcd->btgd", y, pool_map).reshape(B, T, P) * pool_scale
    return y, full[:, -L:]


def retention(q, k, v, state):
    B, T, H, dk = q.shape
    dv = v.shape[-1]
    blk = min(CHUNK, T)
    n = T // blk
    lg = log_gammas()
    i = jnp.arange(blk, dtype=F32)
    diff = i[:, None] - i[None, :]
    dmask = jnp.where(diff[None] >= 0, jnp.exp(lg[:, None, None] * jnp.maximum(diff, 0.0)[None]), 0.0)
    xi = jnp.exp(lg[None, :] * (i[:, None] + 1.0))
    zeta = jnp.exp(lg[None, :] * (blk - 1.0 - i[:, None]))
    g_blk = jnp.exp(lg * blk)

    def to_chunks(a):
        return a.astype(F32).reshape(B, n, blk, H, a.shape[-1]).transpose(1, 0, 2, 3, 4)

    def step(S, inp):
        qc, kc, vc = inp
        sc = jnp.einsum("bihd,bjhd->bhij", qc, kc) * dmask[None]
        inner = jnp.einsum("bhij,bjhe->bihe", sc, vc)
        cross = jnp.einsum("bihd,bhde->bihe", qc, S) * xi[None, :, :, None]
        S = g_blk[None, :, None, None] * S + jnp.einsum("bjhd,bjhe->bhde", kc * zeta[None, :, :, None], vc)
        return S, inner + cross

    S, ys = lax.scan(step, state.astype(F32), (to_chunks(q), to_chunks(k), to_chunks(v)))
    y = ys.transpose(1, 0, 2, 3, 4).reshape(B, T, H, dv)
    return y, S


def layer(x, c, start, pool_buf, ret_state, w_ada, b_ada, norm_g, w_up1, w_down1, w_up2, w_down2,
          w_in, pool_map, pool_scale, w_pool_out, ret_gn, w_ret_out, w_o):
    B, T, D = x.shape
    mod = (jax.nn.silu(c) @ w_ada + b_ada).reshape(B, N_MOD, 1, D)
    h = rmsnorm(x, norm_g[0]) * (1 + mod[:, 1]) + mod[:, 0]
    x = x + 0.5 * mod[:, 2] * swiglu(h, w_up1, w_down1)
    h = rmsnorm(x, norm_g[1]) * (1 + mod[:, 4]) + mod[:, 3]
    z = h @ w_in
    cuts = np.cumsum([POOL_WIDTH, RET_WIDTH, RET_WIDTH, RET_WIDTH, RET_WIDTH]).tolist()
    z_pool, z_q, z_k, z_v, z_g, z_gate = jnp.split(z, cuts, axis=-1)
    pool_y, new_buf = pool_mix(z_pool, pool_buf, start, pool_map, pool_scale)
    pos = start + jnp.arange(T)
    q = rope(z_q.reshape(B, T, RET_HEADS, RET_DK), pos)
    k = rope(z_k.reshape(B, T, RET_HEADS, RET_DK), pos) * (RET_DK ** -0.5)
    v = z_v.reshape(B, T, RET_HEADS, RET_DV)
    y, new_state = retention(q, k, v, ret_state)
    mu = jnp.mean(y, axis=-1, keepdims=True)
    var = jnp.mean(jnp.square(y - mu), axis=-1, keepdims=True)
    yn = ((y - mu) * lax.rsqrt(var + EPS)).reshape(B, T, RET_WIDTH).astype(x.dtype) * ret_gn
    ret_y = jax.nn.silu(z_g) * yn
    gates = jax.nn.sigmoid(z_gate.astype(F32)).astype(x.dtype).reshape(B, T, N_BRANCH, D)
    merged = gates[:, :, 0] * (pool_y @ w_pool_out) + gates[:, :, 1] * (ret_y @ w_ret_out)
    x = x + mod[:, 5] * (merged @ w_o)
    h = rmsnorm(x, norm_g[2]) * (1 + mod[:, 7]) + mod[:, 6]
    x = x + 0.5 * mod[:, 8] * swiglu(h, w_up2, w_down2)
    return x, new_buf, new_state.astype(ret_state.dtype)


def trunk(x, c, start, pool_bufs, ret_states, w_ada, b_ada, norm_g, w_up1, w_down1, w_up2, w_down2,
          w_in, pool_map, pool_scale, w_pool_out, ret_gn, w_ret_out, w_o, final_g):
    bufs, states = [], []
    for l in range(DEPTH):
        x, nb, ns = layer(x, c, start, pool_bufs[l], ret_states[l], w_ada[l], b_ada[l], norm_g[l],
                          w_up1[l], w_down1[l], w_up2[l], w_down2[l], w_in[l], pool_map[l],
                          pool_scale[l], w_pool_out[l], ret_gn[l], w_ret_out[l], w_o[l])
        bufs.append(nb)
        states.append(ns)
    return rmsnorm(x, final_g), jnp.stack(bufs, axis=0), jnp.stack(states, axis=0)


def setup_inputs(seed: int = 0) -> dict:
    key = jax.random.key(seed)
    ks = jax.random.split(key, 24)
    D = D_MODEL

    def nrm(k, shape, s):
        return jax.random.normal(k, shape, jnp.float32) * s

    return {
        "x_prompt": nrm(ks[0], (BATCH, SEQ, D), 1.0),
        "x_sample": nrm(ks[1], (DEC_BATCH, DEC_SEQ, D), 1.0),
        "c_prompt": nrm(ks[2], (BATCH, D), 1.0),
        "c_sample": nrm(ks[3], (DEC_BATCH, D), 1.0),
        "cache_pool": nrm(ks[4], (DEPTH, DEC_BATCH, POOL_BUF, POOL_WIDTH), 1.0),
        "state_ret": nrm(ks[5], (DEPTH, DEC_BATCH, RET_HEADS, RET_DK, RET_DV), 0.3),
        "w_ada": nrm(ks[6], (DEPTH, D, N_MOD * D), 0.5 * D ** -0.5),
        "b_ada": nrm(ks[7], (DEPTH, N_MOD * D), 0.01),
        "norm_g": 1.0 + nrm(ks[8], (DEPTH, 3, D), 0.1),
        "w_up1": nrm(ks[9], (DEPTH, D, 2 * D_FF), D ** -0.5),
        "w_down1": nrm(ks[10], (DEPTH, D_FF, D), D_FF ** -0.5),
        "w_up2": nrm(ks[11], (DEPTH, D, 2 * D_FF), D ** -0.5),
        "w_down2": nrm(ks[12], (DEPTH, D_FF, D), D_FF ** -0.5),
        "w_in": nrm(ks[13], (DEPTH, D, IN_WIDTH), D ** -0.5),
        "pool_map": nrm(ks[14], (DEPTH, POOL_GROUPS, POOL_GD, POOL_GD), POOL_GD ** -0.5),
        "pool_scale": 1.0 + nrm(ks[15], (DEPTH, POOL_WIDTH), 0.1),
        "w_pool_out": nrm(ks[16], (DEPTH, POOL_WIDTH, D), POOL_WIDTH ** -0.5),
        "ret_gn": 1.0 + nrm(ks[17], (DEPTH, RET_WIDTH), 0.1),
        "w_ret_out": nrm(ks[18], (DEPTH, RET_WIDTH, D), RET_WIDTH ** -0.5),
        "w_o": nrm(ks[19], (DEPTH, D, D), D ** -0.5),
        "final_g": 1.0 + nrm(ks[20], (D,), 0.1),
    }


def reference(x_prompt, x_sample, c_prompt, c_sample, cache_pool, state_ret, w_ada, b_ada, norm_g,
              w_up1, w_down1, w_up2, w_down2, w_in, pool_map, pool_scale, w_pool_out, ret_gn,
              w_ret_out, w_o, final_g):
    zero_bufs = jnp.zeros((DEPTH, x_prompt.shape[0], POOL_BUF, POOL_WIDTH), x_prompt.dtype)
    zero_states = jnp.zeros((DEPTH, x_prompt.shape[0], RET_HEADS, RET_DK, RET_DV), F32)
    y_prompt, pool_prompt, ret_prompt = trunk(
        x_prompt, c_prompt, 0, zero_bufs, zero_states, w_ada, b_ada, norm_g, w_up1, w_down1,
        w_up2, w_down2, w_in, pool_map, pool_scale, w_pool_out, ret_gn, w_ret_out, w_o, final_g)
    y_sample, pool_sample, ret_sample = trunk(
        x_sample, c_sample, PAST_LEN, cache_pool, state_ret, w_ada, b_ada, norm_g, w_up1, w_down1,
        w_up2, w_down2, w_in, pool_map, pool_scale, w_pool_out, ret_gn, w_ret_out, w_o, final_g)
    return (y_prompt, y_sample, pool_prompt, ret_prompt, pool_sample, ret_sample)
```

```python
import functools

import jax
import jax.numpy as jnp
from jax import lax
from jax.experimental import pallas as pl
from jax.experimental.pallas import tpu as pltpu

F32 = jnp.float32
BF16 = jnp.bfloat16

D_MODEL = 1024
DEPTH = 4
PAST_LEN = 4096
POOL_WINDOWS = (2, 4, 8, 16)
POOL_WIDTH = D_MODEL // 2
POOL_GD = POOL_WIDTH // len(POOL_WINDOWS)
POOL_BUF = max(POOL_WINDOWS) - 1
HIST_ROWS = 16
RET_HEADS = 4
RET_DK = D_MODEL // 8
RET_WIDTH = RET_HEADS * RET_DK
IN_WIDTH = POOL_WIDTH + 4 * RET_WIDTH + 2 * D_MODEL
D_FF = ((8 * D_MODEL // 3 + 127) // 128) * 128
N_MOD = 9
ROPE_BASE = 10000.0
EPS = 1e-6

MXU_TILE = 256
FFN_CHUNK = MXU_TILE
N_FFN_CHUNKS = D_FF // FFN_CHUNK
TOKEN_TILE = 512
RET_CHUNK = 256
VMEM_LIMIT_BYTES = 56 * 1024 * 1024


def _sigmoid(v):
    return 1.0 / (1.0 + jnp.exp(-v))


def _resident(shape, index_map):
    return pl.BlockSpec(shape, index_map, pipeline_mode=pl.Buffered(1))


def _mod_kernel(c_ref, w_ref, b_ref, o_ref):
    c = c_ref[...]
    s = (c * _sigmoid(c)).astype(BF16)
    w = w_ref[...].astype(BF16)
    o_ref[...] = jnp.dot(s, w, preferred_element_type=F32) + b_ref[...]


def _mod_call(c, w_ada, b_ada):
    nb, d = c.shape
    depth, _, width = w_ada.shape
    tn = 1024
    return pl.pallas_call(
        _mod_kernel,
        grid=(depth, width // tn),
        in_specs=[
            pl.BlockSpec((nb, d), lambda l, n: (0, 0)),
            pl.BlockSpec((None, d, tn), lambda l, n: (l, 0, n)),
            pl.BlockSpec((None, 1, tn), lambda l, n: (l, 0, n)),
        ],
        out_specs=pl.BlockSpec((None, nb, tn), lambda l, n: (l, 0, n)),
        out_shape=jax.ShapeDtypeStruct((depth, nb, width), F32),
        compiler_params=pltpu.CompilerParams(
            dimension_semantics=("arbitrary", "arbitrary"),
            vmem_limit_bytes=VMEM_LIMIT_BYTES),
        name="adaln_mod",
    )(c, w_ada, b_ada.reshape(depth, 1, width))


def _ffn_kernel(x_ref, mod_ref, g_ref, wup_ref, wdn_ref, *rest, final_norm):
    if final_norm:
        fg_ref, o_ref, gbuf = rest
    else:
        o_ref, gbuf = rest
    nb, tt, d = x_ref.shape
    tm = nb * tt
    x = x_ref[...]
    r = lax.rsqrt(jnp.mean(x * x, axis=-1, keepdims=True) + EPS)
    shift = mod_ref[:, :, 0:d]
    scale = mod_ref[:, :, d:2 * d]
    gate = mod_ref[:, :, 2 * d:3 * d]
    h = ((x * r) * g_ref[...]) * (1.0 + scale) + shift
    h = h.reshape(tm, d).astype(BF16)
    for j in range(N_FFN_CHUNKS):
        ab = jnp.dot(h, wup_ref[j], preferred_element_type=F32)
        a = ab[:, :FFN_CHUNK]
        b = ab[:, FFN_CHUNK:]
        gbuf[:, j * FFN_CHUNK:(j + 1) * FFN_CHUNK] = ((a * _sigmoid(a)) * b).astype(BF16)
    y = jnp.dot(gbuf[...], wdn_ref[...], preferred_element_type=F32)
    out = x + (0.5 * gate) * y.reshape(nb, tt, d)
    if final_norm:
        r2 = lax.rsqrt(jnp.mean(out * out, axis=-1, keepdims=True) + EPS)
        out = (out * r2) * fg_ref[...]
    o_ref[...] = out


def _ffn_call(x, mod4, norm_g3, wup, wdn, final_g, *, layer, sub, row0, nb, tt, final_norm):
    bsz, t, d = x.shape
    grid = (bsz // nb, t // tt)
    mod_blk0 = row0 // nb
    in_specs = [
        pl.BlockSpec((nb, tt, d), lambda b, i: (b, i, 0)),
        pl.BlockSpec((None, nb, 1, 3 * d), lambda b, i: (layer, mod_blk0 + b, 0, sub)),
        _resident((None, 1, d), lambda b, i: (layer * 3 + sub, 0, 0)),
        _resident((None, N_FFN_CHUNKS, d, 2 * FFN_CHUNK), lambda b, i: (layer, 0, 0, 0)),
        _resident((None, D_FF, d), lambda b, i: (layer, 0, 0)),
    ]
    args = [x, mod4, norm_g3, wup, wdn]
    if final_norm:
        in_specs.append(_resident((1, d), lambda b, i: (0, 0)))
        args.append(final_g.reshape(1, d))
    return pl.pallas_call(
        functools.partial(_ffn_kernel, final_norm=final_norm),
        grid=grid,
        in_specs=in_specs,
        out_specs=pl.BlockSpec((nb, tt, d), lambda b, i: (b, i, 0)),
        out_shape=jax.ShapeDtypeStruct(x.shape, F32),
        scratch_shapes=[pltpu.VMEM((nb * tt, D_FF), BF16)],
        compiler_params=pltpu.CompilerParams(
            dimension_semantics=("arbitrary", "arbitrary"),
            vmem_limit_bytes=VMEM_LIMIT_BYTES),
        name="ffn_final" if final_norm else "ffn",
    )(*args)


def _mixer_kernel(x_ref, mod_ref, g_ref, cos_ref, sin_ref, dmask_ref, xi_ref, zeta_ref,
                  gblk_ref, win_ref, pmap_ref, pscale_ref, wpo_ref, rgn_ref, wro_ref,
                  wo_ref, *rest, chunk, start, has_state):
    if has_state:
        hist_ref, state_ref, o_ref, nh_ref, ns_ref, s_acc, ext, qs, ks, vs, ys = rest
    else:
        o_ref, nh_ref, ns_ref, s_acc, ext, qs, ks, vs, ys = rest
    tm, d = x_ref.shape
    i = pl.program_id(1)
    last = pl.num_programs(1) - 1

    @pl.when(i == 0)
    def _():
        if has_state:
            s_acc[...] = state_ref[...]
            ext[0:HIST_ROWS, :] = hist_ref[...]
        else:
            s_acc[...] = jnp.zeros(s_acc.shape, F32)
            ext[0:HIST_ROWS, :] = jnp.zeros((HIST_ROWS, POOL_WIDTH), F32)

    x = x_ref[...]
    r = lax.rsqrt(jnp.mean(x * x, axis=-1, keepdims=True) + EPS)
    shift = mod_ref[:, 0:d]
    scale = mod_ref[:, d:2 * d]
    gate = mod_ref[:, 2 * d:3 * d]
    h = (((x * r) * g_ref[...]) * (1.0 + scale) + shift).astype(BF16)

    zp = jnp.dot(h, win_ref[:, 0:POOL_WIDTH], preferred_element_type=F32)
    ext[HIST_ROWS:HIST_ROWS + tm, :] = zp
    pos = start + i * tm + lax.broadcasted_iota(jnp.int32, (tm, 1), 0)
    pooled = []
    for g, w in enumerate(POOL_WINDOWS):
        sl = slice(g * POOL_GD, (g + 1) * POOL_GD)
        s = zp[:, sl]
        for k in range(1, w):
            s = s + ext[HIST_ROWS - k:HIST_ROWS - k + tm, sl]
        cnt = jnp.minimum(w, pos + 1).astype(F32)
        yg = (s / cnt - zp[:, sl]).astype(BF16)
        pooled.append(jnp.dot(yg, pmap_ref[g], preferred_element_type=F32))
    pool_y = jnp.concatenate(pooled, axis=1) * pscale_ref[...]
    br_a = jnp.dot(pool_y.astype(BF16), wpo_ref[...], preferred_element_type=F32)
    new_hist = ext[tm:tm + HIST_ROWS, :]
    ext[0:HIST_ROWS, :] = new_hist

    @pl.when(i == last)
    def _():
        nh_ref[...] = new_hist

    o_q = POOL_WIDTH
    q = jnp.dot(h, win_ref[:, o_q:o_q + RET_WIDTH], preferred_element_type=F32)
    k = jnp.dot(h, win_ref[:, o_q + RET_WIDTH:o_q + 2 * RET_WIDTH], preferred_element_type=F32)
    v = jnp.dot(h, win_ref[:, o_q + 2 * RET_WIDTH:o_q + 3 * RET_WIDTH], preferred_element_type=F32)
    cos = cos_ref[...]
    sin = sin_ref[...]
    for hh in range(RET_HEADS):
        sl = slice(hh * RET_DK, (hh + 1) * RET_DK)
        qh = q[:, sl]
        kh = k[:, sl]
        qs[:, sl] = (qh * cos + pltpu.roll(qh, RET_DK // 2, 1) * sin).astype(BF16)
        ks[:, sl] = (kh * cos + pltpu.roll(kh, RET_DK // 2, 1) * sin) * (RET_DK ** -0.5)
    vs[...] = v.astype(BF16)
    for c in range(tm // chunk):
        rows = slice(c * chunk, (c + 1) * chunk)
        for hh in range(RET_HEADS):
            sl = slice(hh * RET_DK, (hh + 1) * RET_DK)
            qc = qs[rows, sl]
            kf = ks[rows, sl]
            kc = kf.astype(BF16)
            vc = vs[rows, sl]
            sc = lax.dot_general(qc, kc, (((1,), (1,)), ((), ())),
                                 preferred_element_type=F32) * dmask_ref[hh]
            inner = jnp.dot(sc.astype(BF16), vc, preferred_element_type=F32)
            s_h = s_acc[hh]
            cross = jnp.dot(qc, s_h.astype(BF16), preferred_element_type=F32) * xi_ref[hh]
            ys[rows, sl] = inner + cross
            kz = (kf * zeta_ref[hh]).astype(BF16)
            s_acc[hh] = gblk_ref[hh] * s_h + lax.dot_general(
                kz, vc, (((0,), (0,)), ((), ())), preferred_element_type=F32)

    @pl.when(i == last)
    def _():
        ns_ref[...] = s_acc[...]

    zg = jnp.dot(h, win_ref[:, o_q + 3 * RET_WIDTH:o_q + 4 * RET_WIDTH], preferred_element_type=F32)
    normed = []
    for hh in range(RET_HEADS):
        sl = slice(hh * RET_DK, (hh + 1) * RET_DK)
        y = ys[:, sl]
        mu = jnp.mean(y, axis=-1, keepdims=True)
        dlt = y - mu
        var = jnp.mean(dlt * dlt, axis=-1, keepdims=True)
        normed.append(dlt * lax.rsqrt(var + EPS))
    yn = jnp.concatenate(normed, axis=1) * rgn_ref[...]
    ret_y = (zg * _sigmoid(zg)) * yn
    br_b = jnp.dot(ret_y.astype(BF16), wro_ref[...], preferred_element_type=F32)

    o_g = o_q + 4 * RET_WIDTH
    gts = _sigmoid(jnp.dot(h, win_ref[:, o_g:o_g + 2 * d], preferred_element_type=F32))
    merged = gts[:, 0:d] * br_a + gts[:, d:2 * d] * br_b
    o = jnp.dot(merged.astype(BF16), wo_ref[...], preferred_element_type=F32)
    o_ref[...] = x + gate * o


def _retention_tables(chunk):
    lg = jnp.log1p(-jnp.exp2(-5.0 - jnp.arange(RET_HEADS, dtype=F32)))
    i = jnp.arange(chunk, dtype=F32)
    diff = i[:, None] - i[None, :]
    dmask = jnp.where(diff[None] >= 0,
                      jnp.exp(lg[:, None, None] * jnp.maximum(diff, 0.0)[None]), 0.0)
    xi = jnp.exp(lg[:, None] * (i[None, :] + 1.0))
    zeta = jnp.exp(lg[:, None] * (chunk - 1.0 - i[None, :]))
    gblk = jnp.exp(lg * chunk)
    bcast = lambda a: jnp.broadcast_to(a[:, :, None], (RET_HEADS, a.shape[1], RET_DK))
    return dmask, bcast(xi), bcast(zeta), bcast(gblk[:, None])


def _rope_tables(start, t):
    half = RET_DK // 2
    inv = ROPE_BASE ** (-jnp.arange(half, dtype=F32) / half)
    pos = start + jnp.arange(t)
    ang = pos.astype(F32)[:, None] * inv[None, :]
    cos = jnp.cos(ang)
    sin = jnp.sin(ang)
    return jnp.concatenate([cos, cos], axis=1), jnp.concatenate([-sin, sin], axis=1)


def _mixer_call(x, mod4, norm_g3, tables, weights, hist, state, *, layer, row0, tm, chunk, start):
    bsz, t, d = x.shape
    has_state = state is not None
    cos, sin, dmask, xi, zeta, gblk = tables
    win, pmap, pscale, wpo, rgn, wro, wo = weights
    grid = (bsz, t // tm)
    const2 = lambda b, i: (0, 0)
    const3 = lambda b, i: (0, 0, 0)
    lay2 = lambda b, i: (layer, 0, 0)
    in_specs = [
        pl.BlockSpec((None, tm, d), lambda b, i: (b, i, 0)),
        pl.BlockSpec((None, None, 1, 3 * d), lambda b, i: (layer, row0 + b, 0, 1)),
        _resident((None, 1, d), lambda b, i: (layer * 3 + 1, 0, 0)),
        pl.BlockSpec((tm, RET_DK), lambda b, i: (i, 0)),
        pl.BlockSpec((tm, RET_DK), lambda b, i: (i, 0)),
        _resident((RET_HEADS, chunk, chunk), const3),
        _resident((RET_HEADS, chunk, RET_DK), const3),
        _resident((RET_HEADS, chunk, RET_DK), const3),
        _resident((RET_HEADS, 1, RET_DK), const3),
        _resident((None, d, IN_WIDTH), lay2),
        _resident((None, len(POOL_WINDOWS), POOL_GD, POOL_GD), lambda b, i: (layer, 0, 0, 0)),
        _resident((None, 1, POOL_WIDTH), lay2),
        _resident((None, POOL_WIDTH, d), lay2),
        _resident((None, 1, RET_WIDTH), lay2),
        _resident((None, RET_WIDTH, d), lay2),
        _resident((None, d, d), lay2),
    ]
    args = [x, mod4, norm_g3, cos, sin, dmask, xi, zeta, gblk, win, pmap, pscale, wpo, rgn, wro, wo]
    if has_state:
        in_specs += [
            pl.BlockSpec((None, HIST_ROWS, POOL_WIDTH), lambda b, i: (b, 0, 0)),
            pl.BlockSpec((None, RET_HEADS, RET_DK, RET_DK), lambda b, i: (b, 0, 0, 0)),
        ]
        args += [hist, state]
    out_shape = (
        jax.ShapeDtypeStruct(x.shape, F32),
        jax.ShapeDtypeStruct((bsz, HIST_ROWS, POOL_WIDTH), F32),
        jax.ShapeDtypeStruct((bsz, RET_HEADS, RET_DK, RET_DK), F32),
    )
    out_specs = (
        pl.BlockSpec((None, tm, d), lambda b, i: (b, i, 0)),
        pl.BlockSpec((None, HIST_ROWS, POOL_WIDTH), lambda b, i: (b, 0, 0)),
        pl.BlockSpec((None, RET_HEADS, RET_DK, RET_DK), lambda b, i: (b, 0, 0, 0)),
    )
    scratch = [
        pltpu.VMEM((RET_HEADS, RET_DK, RET_DK), F32),
        pltpu.VMEM((HIST_ROWS + tm, POOL_WIDTH), F32),
        pltpu.VMEM((tm, RET_WIDTH), BF16),
        pltpu.VMEM((tm, RET_WIDTH), F32),
        pltpu.VMEM((tm, RET_WIDTH), BF16),
        pltpu.VMEM((tm, RET_WIDTH), F32),
    ]
    return pl.pallas_call(
        functools.partial(_mixer_kernel, chunk=chunk, start=start, has_state=has_state),
        grid=grid,
        in_specs=in_specs,
        out_specs=out_specs,
        out_shape=out_shape,
        scratch_shapes=scratch,
        compiler_params=pltpu.CompilerParams(
            dimension_semantics=("arbitrary", "arbitrary"),
            vmem_limit_bytes=VMEM_LIMIT_BYTES),
        name="mixer_state" if has_state else "mixer",
    )(*args)


def _prep_up(w_up):
    depth, d, _ = w_up.shape
    a = w_up[:, :, :D_FF].reshape(depth, d, N_FFN_CHUNKS, FFN_CHUNK)
    b = w_up[:, :, D_FF:].reshape(depth, d, N_FFN_CHUNKS, FFN_CHUNK)
    ab = jnp.concatenate([a, b], axis=-1).astype(BF16)
    return jnp.transpose(ab, (0, 2, 1, 3))


def _trunk(x, mod4, norm_g3, ffn_w, mix_w, final_g, hists, states, *, row0, start,
           ffn_nb, ffn_tt, mix_tm, chunk):
    t = x.shape[1]
    cos, sin = _rope_tables(start, t)
    tables = (cos, sin) + _retention_tables(chunk)
    wup1, wdn1, wup2, wdn2 = ffn_w
    new_hists, new_states = [], []
    for l in range(DEPTH):
        x = _ffn_call(x, mod4, norm_g3, wup1, wdn1, final_g, layer=l, sub=0, row0=row0,
                      nb=ffn_nb, tt=ffn_tt, final_norm=False)
        x, nh, ns = _mixer_call(
            x, mod4, norm_g3, tables, mix_w,
            None if hists is None else hists[l], None if states is None else states[l],
            layer=l, row0=row0, tm=mix_tm, chunk=chunk, start=start)
        new_hists.append(nh[:, HIST_ROWS - POOL_BUF:, :])
        new_states.append(ns)
        x = _ffn_call(x, mod4, norm_g3, wup2, wdn2, final_g, layer=l, sub=2, row0=row0,
                      nb=ffn_nb, tt=ffn_tt, final_norm=(l == DEPTH - 1))
    return x, jnp.stack(new_hists, axis=0), jnp.stack(new_states, axis=0)


def kernel(x_prompt, x_sample, c_prompt, c_sample, cache_pool, state_ret, w_ada, b_ada, norm_g,
           w_up1, w_down1, w_up2, w_down2, w_in, pool_map, pool_scale, w_pool_out, ret_gn,
           w_ret_out, w_o, final_g):
    d = D_MODEL
    n_prompt = x_prompt.shape[0]
    n_sample = x_sample.shape[0]
    c = jnp.concatenate([c_prompt, c_sample], axis=0)
    mod = _mod_call(c, w_ada, b_ada)
    mod4 = mod.reshape(DEPTH, n_prompt + n_sample, 1, N_MOD * d)
    norm_g3 = norm_g.reshape(DEPTH * 3, 1, d)

    ffn_w = (_prep_up(w_up1), w_down1.astype(BF16), _prep_up(w_up2), w_down2.astype(BF16))
    mix_w = (w_in.astype(BF16), pool_map.astype(BF16), pool_scale.reshape(DEPTH, 1, POOL_WIDTH),
             w_pool_out.astype(BF16), ret_gn.reshape(DEPTH, 1, RET_WIDTH),
             w_ret_out.astype(BF16), w_o.astype(BF16))

    y_prompt, pool_prompt, ret_prompt = _trunk(
        x_prompt, mod4, norm_g3, ffn_w, mix_w, final_g, None, None, row0=0, start=0,
        ffn_nb=1, ffn_tt=TOKEN_TILE, mix_tm=TOKEN_TILE, chunk=RET_CHUNK)

    t_s = x_sample.shape[1]
    hists = jnp.pad(cache_pool, ((0, 0), (0, 0), (HIST_ROWS - POOL_BUF, 0), (0, 0)))
    y_sample, pool_sample, ret_sample = _trunk(
        x_sample, mod4, norm_g3, ffn_w, mix_w, final_g, hists, state_ret, row0=n_prompt,
        start=PAST_LEN, ffn_nb=n_sample, ffn_tt=t_s, mix_tm=t_s, chunk=t_s)
    return (y_prompt, y_sample, pool_prompt, ret_prompt, pool_sample, ret_sample)
```

```python
import functools

import jax
import jax.numpy as jnp
from jax import lax
from jax.experimental import pallas as pl
from jax.experimental.pallas import tpu as pltpu

F32 = jnp.float32
BF16 = jnp.bfloat16

D_MODEL = 1024
DEPTH = 4
PAST_LEN = 4096
POOL_WINDOWS = (2, 4, 8, 16)
POOL_WIDTH = D_MODEL // 2
POOL_GD = POOL_WIDTH // len(POOL_WINDOWS)
POOL_BUF = max(POOL_WINDOWS) - 1
SUBLANES = 8
HIST_ROWS = 16
POOL_HEAD = HIST_ROWS + SUBLANES
RET_HEADS = 4
RET_DK = D_MODEL // 8
RET_WIDTH = RET_HEADS * RET_DK
IN_WIDTH = POOL_WIDTH + 4 * RET_WIDTH + 2 * D_MODEL
D_FF = ((8 * D_MODEL // 3 + 127) // 128) * 128
N_MOD = 9
ROPE_BASE = 10000.0
EPS = 1e-6

MXU_TILE = 256
FFN_CHUNK = MXU_TILE
N_FFN_CHUNKS = D_FF // FFN_CHUNK
TOKEN_TILE = 512
FFN_SUBTILES = 2
RET_CHUNK = 256
VMEM_LIMIT_BYTES = 56 * 1024 * 1024


def _sigmoid(v):
    return 1.0 / (1.0 + jnp.exp(-v))


def _resident(shape, index_map):
    return pl.BlockSpec(shape, index_map, pipeline_mode=pl.Buffered(1))


def _mod_kernel(c_ref, w_ref, b_ref, o_ref):
    c = c_ref[...]
    s = (c * _sigmoid(c)).astype(BF16)
    w = w_ref[...].astype(BF16)
    o_ref[...] = jnp.dot(s, w, preferred_element_type=F32) + b_ref[...]


def _mod_call(c, w_ada, b_ada):
    nb, d = c.shape
    depth, _, width = w_ada.shape
    tn = 1024
    return pl.pallas_call(
        _mod_kernel,
        grid=(depth, width // tn),
        in_specs=[
            pl.BlockSpec((nb, d), lambda l, n: (0, 0)),
            pl.BlockSpec((None, d, tn), lambda l, n: (l, 0, n)),
            pl.BlockSpec((None, 1, tn), lambda l, n: (l, 0, n)),
        ],
        out_specs=pl.BlockSpec((None, nb, tn), lambda l, n: (l, 0, n)),
        out_shape=jax.ShapeDtypeStruct((depth, nb, width), F32),
        compiler_params=pltpu.CompilerParams(
            dimension_semantics=("arbitrary", "arbitrary"),
            vmem_limit_bytes=VMEM_LIMIT_BYTES),
        name="adaln_mod",
    )(c, w_ada, b_ada.reshape(depth, 1, width))


def _ffn_kernel(x_ref, mod_ref, g_ref, wup_ref, wdn_ref, *rest, final_norm, n_sub):
    if final_norm:
        fg_ref, o_ref, gbuf = rest
    else:
        o_ref, gbuf = rest
    nb, tt, d = x_ref.shape
    shift = mod_ref[:, :, 0:d]
    scale = mod_ref[:, :, d:2 * d]
    gate = mod_ref[:, :, 2 * d:3 * d]
    ts = tt // n_sub
    tm = nb * ts
    for s in range(n_sub):
        x = x_ref[:, s * ts:(s + 1) * ts, :]
        r = lax.rsqrt(jnp.mean(x * x, axis=-1, keepdims=True) + EPS)
        h = ((x * r) * g_ref[...]) * (1.0 + scale) + shift
        h = h.reshape(tm, d).astype(BF16)
        for j in range(N_FFN_CHUNKS):
            c0 = j * FFN_CHUNK
            a = jnp.dot(h, wup_ref[:, c0:c0 + FFN_CHUNK], preferred_element_type=F32)
            b = jnp.dot(h, wup_ref[:, D_FF + c0:D_FF + c0 + FFN_CHUNK],
                        preferred_element_type=F32)
            gbuf[s * tm:(s + 1) * tm, j * FFN_CHUNK:(j + 1) * FFN_CHUNK] = (
                (a * _sigmoid(a)) * b).astype(BF16)
        y = jnp.dot(gbuf[s * tm:(s + 1) * tm, :], wdn_ref[...], preferred_element_type=F32)
        out = x + (0.5 * gate) * y.reshape(nb, ts, d)
        if final_norm:
            r2 = lax.rsqrt(jnp.mean(out * out, axis=-1, keepdims=True) + EPS)
            out = (out * r2) * fg_ref[...]
        o_ref[:, s * ts:(s + 1) * ts, :] = out


def _ffn_call(x, mod4, norm_g3, wup, wdn, final_g, *, layer, sub, row0, nb, tt, final_norm):
    bsz, t, d = x.shape
    grid = (bsz // nb, t // tt)
    mod_blk0 = row0 // nb
    in_specs = [
        pl.BlockSpec((nb, tt, d), lambda b, i: (b, i, 0)),
        pl.BlockSpec((None, nb, 1, 3 * d), lambda b, i: (layer, mod_blk0 + b, 0, sub)),
        _resident((None, 1, d), lambda b, i: (layer * 3 + sub, 0, 0)),
        _resident((None, d, 2 * D_FF), lambda b, i: (layer, 0, 0)),
        _resident((None, D_FF, d), lambda b, i: (layer, 0, 0)),
    ]
    args = [x, mod4, norm_g3, wup, wdn]
    if final_norm:
        in_specs.append(_resident((1, d), lambda b, i: (0, 0)))
        args.append(final_g.reshape(1, d))
    return pl.pallas_call(
        functools.partial(_ffn_kernel, final_norm=final_norm,
                          n_sub=max(1, nb * tt // TOKEN_TILE)),
        grid=grid,
        in_specs=in_specs,
        out_specs=pl.BlockSpec((nb, tt, d), lambda b, i: (b, i, 0)),
        out_shape=jax.ShapeDtypeStruct(x.shape, F32),
        scratch_shapes=[pltpu.VMEM((nb * tt, D_FF), BF16)],
        compiler_params=pltpu.CompilerParams(
            dimension_semantics=("arbitrary", "arbitrary"),
            vmem_limit_bytes=VMEM_LIMIT_BYTES),
        name="ffn_final" if final_norm else "ffn",
    )(*args)


def _mixer_kernel(x_ref, mod_ref, g_ref, cos_ref, sin_ref, dmask_ref, xi_ref, zeta_ref,
                  gblk_ref, win_ref, pmap_ref, pscale_ref, wpo_ref, rgn_ref, wro_ref,
                  wo_ref, *rest, chunk, start, has_state):
    if has_state:
        hist_ref, state_ref, o_ref, nh_ref, ns_ref, s_acc, ext, sum2, sum4, qs, ks, vs, ys = rest
    else:
        o_ref, nh_ref, ns_ref, s_acc, ext, sum2, sum4, qs, ks, vs, ys = rest
    tm, d = x_ref.shape
    i = pl.program_id(1)
    gd = POOL_GD

    @pl.when(i == 0)
    def _():
        ext[0:POOL_HEAD - HIST_ROWS, :] = jnp.zeros((POOL_HEAD - HIST_ROWS, POOL_WIDTH), F32)
        sum2[0:SUBLANES, :] = jnp.zeros((SUBLANES, 3 * gd), F32)
        sum4[0:SUBLANES, :] = jnp.zeros((SUBLANES, 2 * gd), F32)
        if has_state:
            s_acc[...] = state_ref[...]
            ext[POOL_HEAD - HIST_ROWS:POOL_HEAD, :] = hist_ref[...]
        else:
            s_acc[...] = jnp.zeros(s_acc.shape, F32)
            ext[POOL_HEAD - HIST_ROWS:POOL_HEAD, :] = jnp.zeros((HIST_ROWS, POOL_WIDTH), F32)

    x = x_ref[...]
    r = lax.rsqrt(jnp.mean(x * x, axis=-1, keepdims=True) + EPS)
    shift = mod_ref[:, 0:d]
    scale = mod_ref[:, d:2 * d]
    gate = mod_ref[:, 2 * d:3 * d]
    h = (((x * r) * g_ref[...]) * (1.0 + scale) + shift).astype(BF16)

    zp = jnp.dot(h, win_ref[:, 0:POOL_WIDTH], preferred_element_type=F32)
    ext[POOL_HEAD:POOL_HEAD + tm, :] = zp
    n = tm + HIST_ROWS
    w2 = ext[SUBLANES:SUBLANES + n, gd:] + ext[SUBLANES - 1:SUBLANES - 1 + n, gd:]
    sum2[SUBLANES:SUBLANES + n, :] = w2
    w4 = w2 + sum2[SUBLANES - 2:SUBLANES - 2 + n, :]
    sum4[SUBLANES:SUBLANES + n, :] = w4[:, gd:]
    w8 = w4[:, gd:] + sum4[SUBLANES - 4:SUBLANES - 4 + n, :]
    t0 = POOL_HEAD - SUBLANES
    sums = (
        zp[:, 0:gd] + ext[POOL_HEAD - 1:POOL_HEAD - 1 + tm, 0:gd],
        w4[t0:t0 + tm, 0:gd],
        w8[t0:t0 + tm, 0:gd],
        w8[t0:t0 + tm, gd:] + w8[t0 - SUBLANES:t0 - SUBLANES + tm, gd:],
    )
    pooled = []
    for g, w in enumerate(POOL_WINDOWS):
        s = sums[g]
        mean = s * (1.0 / w)
        if start < POOL_BUF:
            pos = start + i * tm + lax.broadcasted_iota(jnp.int32, (HIST_ROWS, gd), 0)
            inv = 1.0 / jnp.minimum(w, pos + 1).astype(F32)
            mean = jnp.concatenate([s[0:HIST_ROWS] * inv, mean[HIST_ROWS:]], axis=0)
        yg = (mean - zp[:, g * gd:(g + 1) * gd]).astype(BF16)
        pooled.append(jnp.dot(yg, pmap_ref[g], preferred_element_type=F32))
    pool_y = jnp.concatenate(pooled, axis=1) * pscale_ref[...]
    br_a = jnp.dot(pool_y.astype(BF16), wpo_ref[...], preferred_element_type=F32)
    new_hist = ext[tm + POOL_HEAD - HIST_ROWS:tm + POOL_HEAD, :]
    ext[POOL_HEAD - HIST_ROWS:POOL_HEAD, :] = new_hist
    nh_ref[...] = new_hist

    o_q = POOL_WIDTH
    q = jnp.dot(h, win_ref[:, o_q:o_q + RET_WIDTH], preferred_element_type=F32)
    k = jnp.dot(h, win_ref[:, o_q + RET_WIDTH:o_q + 2 * RET_WIDTH], preferred_element_type=F32)
    v = jnp.dot(h, win_ref[:, o_q + 2 * RET_WIDTH:o_q + 3 * RET_WIDTH], preferred_element_type=F32)
    cos = cos_ref[...]
    sin = sin_ref[...]
    for hh in range(RET_HEADS):
        sl = slice(hh * RET_DK, (hh + 1) * RET_DK)
        qh = q[:, sl]
        kh = k[:, sl]
        qs[:, sl] = (qh * cos + pltpu.roll(qh, RET_DK // 2, 1) * sin).astype(BF16)
        ks[:, sl] = (kh * cos + pltpu.roll(kh, RET_DK // 2, 1) * sin) * (RET_DK ** -0.5)
    vs[...] = v.astype(BF16)
    for c in range(tm // chunk):
        rows = slice(c * chunk, (c + 1) * chunk)
        for hh in range(RET_HEADS):
            sl = slice(hh * RET_DK, (hh + 1) * RET_DK)
            qc = qs[rows, sl]
            kf = ks[rows, sl]
            kc = kf.astype(BF16)
            vc = vs[rows, sl]
            sc = lax.dot_general(qc, kc, (((1,), (1,)), ((), ())),
                                 preferred_element_type=F32) * dmask_ref[hh]
            inner = jnp.dot(sc.astype(BF16), vc, preferred_element_type=F32)
            s_h = s_acc[hh]
            cross = jnp.dot(qc, s_h.astype(BF16), preferred_element_type=F32) * xi_ref[hh]
            ys[rows, sl] = inner + cross
            kz = (kf * zeta_ref[hh]).astype(BF16)
            s_acc[hh] = gblk_ref[hh] * s_h + lax.dot_general(
                kz, vc, (((0,), (0,)), ((), ())), preferred_element_type=F32)

    ns_ref[...] = s_acc[...]

    zg = jnp.dot(h, win_ref[:, o_q + 3 * RET_WIDTH:o_q + 4 * RET_WIDTH], preferred_element_type=F32)
    normed = []
    for hh in range(RET_HEADS):
        sl = slice(hh * RET_DK, (hh + 1) * RET_DK)
        y = ys[:, sl]
        mu = jnp.mean(y, axis=-1, keepdims=True)
        dlt = y - mu
        var = jnp.mean(dlt * dlt, axis=-1, keepdims=True)
        normed.append(dlt * lax.rsqrt(var + EPS))
    yn = jnp.concatenate(normed, axis=1) * rgn_ref[...]
    ret_y = (zg * _sigmoid(zg)) * yn
    br_b = jnp.dot(ret_y.astype(BF16), wro_ref[...], preferred_element_type=F32)

    o_g = o_q + 4 * RET_WIDTH
    gts = _sigmoid(jnp.dot(h, win_ref[:, o_g:o_g + 2 * d], preferred_element_type=F32))
    merged = gts[:, 0:d] * br_a + gts[:, d:2 * d] * br_b
    o = jnp.dot(merged.astype(BF16), wo_ref[...], preferred_element_type=F32)
    o_ref[...] = x + gate * o


def _retention_tables(chunk):
    lg = jnp.log1p(-jnp.exp2(-5.0 - jnp.arange(RET_HEADS, dtype=F32)))
    i = jnp.arange(chunk, dtype=F32)
    diff = i[:, None] - i[None, :]
    dmask = jnp.where(diff[None] >= 0,
                      jnp.exp(lg[:, None, None] * jnp.maximum(diff, 0.0)[None]), 0.0)
    xi = jnp.exp(lg[:, None] * (i[None, :] + 1.0))
    zeta = jnp.exp(lg[:, None] * (chunk - 1.0 - i[None, :]))
    gblk = jnp.exp(lg * chunk)
    bcast = lambda a: jnp.broadcast_to(a[:, :, None], (RET_HEADS, a.shape[1], RET_DK))
    return dmask, bcast(xi), bcast(zeta), bcast(gblk[:, None])


def _rope_tables(start, t):
    half = RET_DK // 2
    inv = ROPE_BASE ** (-jnp.arange(half, dtype=F32) / half)
    pos = start + jnp.arange(t)
    ang = pos.astype(F32)[:, None] * inv[None, :]
    cos = jnp.cos(ang)
    sin = jnp.sin(ang)
    return jnp.concatenate([cos, cos], axis=1), jnp.concatenate([-sin, sin], axis=1)


def _mixer_call(x, mod4, norm_g3, tables, weights, hist, state, *, layer, row0, tm, chunk, start):
    bsz, t, d = x.shape
    has_state = state is not None
    cos, sin, dmask, xi, zeta, gblk = tables
    win, pmap, pscale, wpo, rgn, wro, wo = weights
    grid = (bsz, t // tm)
    const3 = lambda b, i: (0, 0, 0)
    lay2 = lambda b, i: (layer, 0, 0)
    in_specs = [
        pl.BlockSpec((None, tm, d), lambda b, i: (b, i, 0)),
        pl.BlockSpec((None, None, 1, 3 * d), lambda b, i: (layer, row0 + b, 0, 1)),
        _resident((None, 1, d), lambda b, i: (layer * 3 + 1, 0, 0)),
        pl.BlockSpec((tm, RET_DK), lambda b, i: (i, 0)),
        pl.BlockSpec((tm, RET_DK), lambda b, i: (i, 0)),
        _resident((RET_HEADS, chunk, chunk), const3),
        _resident((RET_HEADS, chunk, RET_DK), const3),
        _resident((RET_HEADS, chunk, RET_DK), const3),
        _resident((RET_HEADS, 1, RET_DK), const3),
        _resident((None, d, IN_WIDTH), lay2),
        _resident((None, len(POOL_WINDOWS), POOL_GD, POOL_GD), lambda b, i: (layer, 0, 0, 0)),
        _resident((None, 1, POOL_WIDTH), lay2),
        _resident((None, POOL_WIDTH, d), lay2),
        _resident((None, 1, RET_WIDTH), lay2),
        _resident((None, RET_WIDTH, d), lay2),
        _resident((None, d, d), lay2),
    ]
    args = [x, mod4, norm_g3, cos, sin, dmask, xi, zeta, gblk, win, pmap, pscale, wpo, rgn, wro, wo]
    if has_state:
        in_specs += [
            pl.BlockSpec((None, HIST_ROWS, POOL_WIDTH), lambda b, i: (b, 0, 0)),
            pl.BlockSpec((None, RET_HEADS, RET_DK, RET_DK), lambda b, i: (b, 0, 0, 0)),
        ]
        args += [hist, state]
    out_shape = (
        jax.ShapeDtypeStruct(x.shape, F32),
        jax.ShapeDtypeStruct((bsz, HIST_ROWS, POOL_WIDTH), F32),
        jax.ShapeDtypeStruct((bsz, RET_HEADS, RET_DK, RET_DK), F32),
    )
    out_specs = (
        pl.BlockSpec((None, tm, d), lambda b, i: (b, i, 0)),
        pl.BlockSpec((None, HIST_ROWS, POOL_WIDTH), lambda b, i: (b, 0, 0)),
        pl.BlockSpec((None, RET_HEADS, RET_DK, RET_DK), lambda b, i: (b, 0, 0, 0)),
    )
    scratch = [
        pltpu.VMEM((RET_HEADS, RET_DK, RET_DK), F32),
        pltpu.VMEM((POOL_HEAD + tm, POOL_WIDTH), F32),
        pltpu.VMEM((POOL_HEAD + tm, 3 * POOL_GD), F32),
        pltpu.VMEM((POOL_HEAD + tm, 2 * POOL_GD), F32),
        pltpu.VMEM((tm, RET_WIDTH), BF16),
        pltpu.VMEM((tm, RET_WIDTH), F32),
        pltpu.VMEM((tm, RET_WIDTH), BF16),
        pltpu.VMEM((tm, RET_WIDTH), F32),
    ]
    return pl.pallas_call(
        functools.partial(_mixer_kernel, chunk=chunk, start=start, has_state=has_state),
        grid=grid,
        in_specs=in_specs,
        out_specs=out_specs,
        out_shape=out_shape,
        scratch_shapes=scratch,
        compiler_params=pltpu.CompilerParams(
            dimension_semantics=("arbitrary", "arbitrary"),
            vmem_limit_bytes=VMEM_LIMIT_BYTES),
        name="mixer_state" if has_state else "mixer",
    )(*args)


def _trunk(x, mod4, norm_g3, ffn_w, mix_w, final_g, hists, states, *, row0, start,
           ffn_nb, ffn_tt, mix_tm, chunk):
    t = x.shape[1]
    cos, sin = _rope_tables(start, t)
    tables = (cos, sin) + _retention_tables(chunk)
    wup1, wdn1, wup2, wdn2 = ffn_w
    new_hists, new_states = [], []
    for l in range(DEPTH):
        x = _ffn_call(x, mod4, norm_g3, wup1, wdn1, final_g, layer=l, sub=0, row0=row0,
                      nb=ffn_nb, tt=ffn_tt, final_norm=False)
        x, nh, ns = _mixer_call(
            x, mod4, norm_g3, tables, mix_w,
            None if hists is None else hists[l], None if states is None else states[l],
            layer=l, row0=row0, tm=mix_tm, chunk=chunk, start=start)
        new_hists.append(nh[:, HIST_ROWS - POOL_BUF:, :])
        new_states.append(ns)
        x = _ffn_call(x, mod4, norm_g3, wup2, wdn2, final_g, layer=l, sub=2, row0=row0,
                      nb=ffn_nb, tt=ffn_tt, final_norm=(l == DEPTH - 1))
    return x, jnp.stack(new_hists, axis=0), jnp.stack(new_states, axis=0)


def kernel(x_prompt, x_sample, c_prompt, c_sample, cache_pool, state_ret, w_ada, b_ada, norm_g,
           w_up1, w_down1, w_up2, w_down2, w_in, pool_map, pool_scale, w_pool_out, ret_gn,
           w_ret_out, w_o, final_g):
    d = D_MODEL
    n_prompt = x_prompt.shape[0]
    n_sample = x_sample.shape[0]
    c = jnp.concatenate([c_prompt, c_sample], axis=0)
    mod = _mod_call(c, w_ada, b_ada)
    mod4 = mod.reshape(DEPTH, n_prompt + n_sample, 1, N_MOD * d)
    norm_g3 = norm_g.reshape(DEPTH * 3, 1, d)

    ffn_w = (w_up1.astype(BF16), w_down1.astype(BF16), w_up2.astype(BF16), w_down2.astype(BF16))
    mix_w = (w_in.astype(BF16), pool_map.astype(BF16), pool_scale.reshape(DEPTH, 1, POOL_WIDTH),
             w_pool_out.astype(BF16), ret_gn.reshape(DEPTH, 1, RET_WIDTH),
             w_ret_out.astype(BF16), w_o.astype(BF16))

    y_prompt, pool_prompt, ret_prompt = _trunk(
        x_prompt, mod4, norm_g3, ffn_w, mix_w, final_g, None, None, row0=0, start=0,
        ffn_nb=1, ffn_tt=FFN_SUBTILES * TOKEN_TILE, mix_tm=TOKEN_TILE, chunk=RET_CHUNK)

    t_s = x_sample.shape[1]
    hists = jnp.pad(cache_pool, ((0, 0), (0, 0), (HIST_ROWS - POOL_BUF, 0), (0, 0)))
    y_sample, pool_sample, ret_sample = _trunk(
        x_sample, mod4, norm_g3, ffn_w, mix_w, final_g, hists, state_ret, row0=n_prompt,
        start=PAST_LEN, ffn_nb=n_sample, ffn_tt=t_s, mix_tm=t_s, chunk=t_s)
    return (y_prompt, y_sample, pool_prompt, ret_prompt, pool_sample, ret_sample)
```

```python
import functools

import jax
import jax.numpy as jnp
from jax import lax
from jax.experimental import pallas as pl
from jax.experimental.pallas import tpu as pltpu

F32 = jnp.float32
BF16 = jnp.bfloat16

D_MODEL = 1024
DEPTH = 4
PAST_LEN = 4096
POOL_WINDOWS = (2, 4, 8, 16)
POOL_WIDTH = D_MODEL // 2
POOL_GD = POOL_WIDTH // len(POOL_WINDOWS)
POOL_BUF = max(POOL_WINDOWS) - 1
SUBLANES = 8
HIST_ROWS = 16
POOL_HEAD = HIST_ROWS + SUBLANES
RET_HEADS = 4
RET_DK = D_MODEL // 8
RET_WIDTH = RET_HEADS * RET_DK
IN_WIDTH = POOL_WIDTH + 4 * RET_WIDTH + 2 * D_MODEL
D_FF = ((8 * D_MODEL // 3 + 127) // 128) * 128
N_MOD = 9
ROPE_BASE = 10000.0
EPS = 1e-6

MXU_TILE = 256
FFN_CHUNK = MXU_TILE
N_FFN_CHUNKS = D_FF // FFN_CHUNK
TOKEN_TILE = 512
FFN_SUBTILES = 2
MIXER_SUBTILES = 2
RET_CHUNK = 256
VMEM_LIMIT_BYTES = 56 * 1024 * 1024


def _sigmoid(v):
    return 1.0 / (1.0 + jnp.exp(-v))


def _resident(shape, index_map):
    return pl.BlockSpec(shape, index_map, pipeline_mode=pl.Buffered(1))


def _mod_kernel(c_ref, w_ref, b_ref, o_ref):
    c = c_ref[...]
    s = (c * _sigmoid(c)).astype(BF16)
    w = w_ref[...].astype(BF16)
    o_ref[...] = jnp.dot(s, w, preferred_element_type=F32) + b_ref[...]


def _mod_call(c, w_ada, b_ada):
    nb, d = c.shape
    depth, _, width = w_ada.shape
    tn = 1024
    return pl.pallas_call(
        _mod_kernel,
        grid=(depth, width // tn),
        in_specs=[
            pl.BlockSpec((nb, d), lambda l, n: (0, 0)),
            pl.BlockSpec((None, d, tn), lambda l, n: (l, 0, n)),
            pl.BlockSpec((None, 1, tn), lambda l, n: (l, 0, n)),
        ],
        out_specs=pl.BlockSpec((None, nb, tn), lambda l, n: (l, 0, n)),
        out_shape=jax.ShapeDtypeStruct((depth, nb, width), F32),
        compiler_params=pltpu.CompilerParams(
            dimension_semantics=("arbitrary", "arbitrary"),
            vmem_limit_bytes=VMEM_LIMIT_BYTES),
        name="adaln_mod",
    )(c, w_ada, b_ada.reshape(depth, 1, width))


def _ffn_kernel(x_ref, mod_ref, g_ref, wup_ref, wdn_ref, *rest, final_norm, n_sub):
    if final_norm:
        fg_ref, o_ref, gbuf = rest
    else:
        o_ref, gbuf = rest
    nb, tt, d = x_ref.shape
    shift = mod_ref[:, :, 0:d]
    scale = mod_ref[:, :, d:2 * d]
    gate = mod_ref[:, :, 2 * d:3 * d]
    ts = tt // n_sub
    tm = nb * ts
    for s in range(n_sub):
        x = x_ref[:, s * ts:(s + 1) * ts, :]
        r = lax.rsqrt(jnp.mean(x * x, axis=-1, keepdims=True) + EPS)
        h = ((x * r) * g_ref[...]) * (1.0 + scale) + shift
        h = h.reshape(tm, d).astype(BF16)
        for j in range(N_FFN_CHUNKS):
            c0 = j * FFN_CHUNK
            a = jnp.dot(h, wup_ref[:, c0:c0 + FFN_CHUNK], preferred_element_type=F32)
            b = jnp.dot(h, wup_ref[:, D_FF + c0:D_FF + c0 + FFN_CHUNK],
                        preferred_element_type=F32)
            gbuf[s * tm:(s + 1) * tm, j * FFN_CHUNK:(j + 1) * FFN_CHUNK] = (
                (a * _sigmoid(a)) * b).astype(BF16)
        y = jnp.dot(gbuf[s * tm:(s + 1) * tm, :], wdn_ref[...], preferred_element_type=F32)
        out = x + (0.5 * gate) * y.reshape(nb, ts, d)
        if final_norm:
            r2 = lax.rsqrt(jnp.mean(out * out, axis=-1, keepdims=True) + EPS)
            out = (out * r2) * fg_ref[...]
        o_ref[:, s * ts:(s + 1) * ts, :] = out


def _ffn_call(x, mod4, norm_g3, wup, wdn, final_g, *, layer, sub, row0, nb, tt, final_norm):
    bsz, t, d = x.shape
    grid = (bsz // nb, t // tt)
    mod_blk0 = row0 // nb
    in_specs = [
        pl.BlockSpec((nb, tt, d), lambda b, i: (b, i, 0)),
        pl.BlockSpec((None, nb, 1, 3 * d), lambda b, i: (layer, mod_blk0 + b, 0, sub)),
        _resident((None, 1, d), lambda b, i: (layer * 3 + sub, 0, 0)),
        _resident((None, d, 2 * D_FF), lambda b, i: (layer, 0, 0)),
        _resident((None, D_FF, d), lambda b, i: (layer, 0, 0)),
    ]
    args = [x, mod4, norm_g3, wup, wdn]
    if final_norm:
        in_specs.append(_resident((1, d), lambda b, i: (0, 0)))
        args.append(final_g.reshape(1, d))
    return pl.pallas_call(
        functools.partial(_ffn_kernel, final_norm=final_norm,
                          n_sub=max(1, nb * tt // TOKEN_TILE)),
        grid=grid,
        in_specs=in_specs,
        out_specs=pl.BlockSpec((nb, tt, d), lambda b, i: (b, i, 0)),
        out_shape=jax.ShapeDtypeStruct(x.shape, F32),
        scratch_shapes=[pltpu.VMEM((nb * tt, D_FF), BF16)],
        compiler_params=pltpu.CompilerParams(
            dimension_semantics=("arbitrary", "arbitrary"),
            vmem_limit_bytes=VMEM_LIMIT_BYTES),
        name="ffn_final" if final_norm else "ffn",
    )(*args)


def _split_bf16(a):
    hi = a.astype(BF16)
    return hi, (a - hi.astype(F32)).astype(BF16)


def _pool_fold_kernel(pm_ref, ps_ref, wpo_ref, o_ref):
    a_hi, a_lo = _split_bf16(pm_ref[...] * ps_ref[...])
    b_hi, b_lo = _split_bf16(wpo_ref[...])
    acc = jnp.dot(a_lo, b_hi, preferred_element_type=F32)
    acc += jnp.dot(a_hi, b_lo, preferred_element_type=F32)
    acc += jnp.dot(a_hi, b_hi, preferred_element_type=F32)
    o_ref[...] = acc.astype(BF16)


def _pool_fold_call(pool_map, pool_scale, w_pool_out):
    depth, groups, gd, _ = pool_map.shape
    d = w_pool_out.shape[2]
    return pl.pallas_call(
        _pool_fold_kernel,
        grid=(depth, groups),
        in_specs=[
            pl.BlockSpec((None, None, gd, gd), lambda l, g: (l, g, 0, 0)),
            pl.BlockSpec((None, None, 1, gd), lambda l, g: (l, g, 0, 0)),
            pl.BlockSpec((None, gd, d), lambda l, g: (l, g, 0)),
        ],
        out_specs=pl.BlockSpec((None, gd, d), lambda l, g: (l, g, 0)),
        out_shape=jax.ShapeDtypeStruct((depth, groups * gd, d), BF16),
        compiler_params=pltpu.CompilerParams(dimension_semantics=("arbitrary", "arbitrary")),
        name="pool_fold",
    )(pool_map, pool_scale.reshape(depth, groups, 1, gd), w_pool_out)


def _mixer_kernel(x_ref, mod_ref, g_ref, cos_ref, sin_ref, dmask_ref, xi_ref, zeta_ref,
                  gblk_ref, win_ref, wpo_ref, rgn_ref, wro_ref,
                  wo_ref, *rest, chunk, start, has_state, n_sub):
    if has_state:
        hist_ref, state_ref, o_ref, nh_ref, ns_ref, s_acc, ext, sum2, sum4, qs, ks, vs, ys = rest
    else:
        o_ref, nh_ref, ns_ref, s_acc, ext, sum2, sum4, qs, ks, vs, ys = rest
    tt, d = x_ref.shape
    tm = tt // n_sub
    i = pl.program_id(1)
    gd = POOL_GD

    @pl.when(i == 0)
    def _():
        ext[0:POOL_HEAD - HIST_ROWS, :] = jnp.zeros((POOL_HEAD - HIST_ROWS, POOL_WIDTH), F32)
        sum2[0:SUBLANES, :] = jnp.zeros((SUBLANES, 3 * gd), F32)
        sum4[0:SUBLANES, :] = jnp.zeros((SUBLANES, 2 * gd), F32)
        if has_state:
            s_acc[...] = state_ref[...]
            ext[POOL_HEAD - HIST_ROWS:POOL_HEAD, :] = hist_ref[...]
        else:
            s_acc[...] = jnp.zeros(s_acc.shape, F32)
            ext[POOL_HEAD - HIST_ROWS:POOL_HEAD, :] = jnp.zeros((HIST_ROWS, POOL_WIDTH), F32)

    refs = (x_ref, mod_ref, g_ref, cos_ref, sin_ref, dmask_ref, xi_ref, zeta_ref, gblk_ref,
            win_ref, wpo_ref, rgn_ref, wro_ref, wo_ref, o_ref,
            s_acc, ext, sum2, sum4, qs, ks, vs, ys)
    for sub in range(n_sub):
        _mixer_subtile(refs, sub * tm, i * n_sub + sub, tm=tm, chunk=chunk, start=start)
    nh_ref[...] = ext[POOL_HEAD - HIST_ROWS:POOL_HEAD, :]
    ns_ref[...] = s_acc[...]


def _mixer_subtile(refs, r0, tile, *, tm, chunk, start):
    (x_ref, mod_ref, g_ref, cos_ref, sin_ref, dmask_ref, xi_ref, zeta_ref, gblk_ref,
     win_ref, wpo_ref, rgn_ref, wro_ref, wo_ref, o_ref,
     s_acc, ext, sum2, sum4, qs, ks, vs, ys) = refs
    d = x_ref.shape[1]
    gd = POOL_GD
    x = x_ref[r0:r0 + tm, :]
    r = lax.rsqrt(jnp.mean(x * x, axis=-1, keepdims=True) + EPS)
    shift = mod_ref[:, 0:d]
    scale = mod_ref[:, d:2 * d]
    gate = mod_ref[:, 2 * d:3 * d]
    h = (((x * r) * g_ref[...]) * (1.0 + scale) + shift).astype(BF16)

    zp = jnp.dot(h, win_ref[:, 0:POOL_WIDTH], preferred_element_type=F32)
    ext[POOL_HEAD:POOL_HEAD + tm, :] = zp
    n = tm + HIST_ROWS
    w2 = ext[SUBLANES:SUBLANES + n, gd:] + ext[SUBLANES - 1:SUBLANES - 1 + n, gd:]
    sum2[SUBLANES:SUBLANES + n, :] = w2
    w4 = w2 + sum2[SUBLANES - 2:SUBLANES - 2 + n, :]
    sum4[SUBLANES:SUBLANES + n, :] = w4[:, gd:]
    w8 = w4[:, gd:] + sum4[SUBLANES - 4:SUBLANES - 4 + n, :]
    t0 = POOL_HEAD - SUBLANES
    sums = (
        zp[:, 0:gd] + ext[POOL_HEAD - 1:POOL_HEAD - 1 + tm, 0:gd],
        w4[t0:t0 + tm, 0:gd],
        w8[t0:t0 + tm, 0:gd],
        w8[t0:t0 + tm, gd:] + w8[t0 - SUBLANES:t0 - SUBLANES + tm, gd:],
    )
    pooled = []
    for g, w in enumerate(POOL_WINDOWS):
        s = sums[g]
        mean = s * (1.0 / w)
        if start < POOL_BUF:
            pos = start + tile * tm + lax.broadcasted_iota(jnp.int32, (HIST_ROWS, gd), 0)
            inv = 1.0 / jnp.minimum(w, pos + 1).astype(F32)
            mean = jnp.concatenate([s[0:HIST_ROWS] * inv, mean[HIST_ROWS:]], axis=0)
        yg = (mean - zp[:, g * gd:(g + 1) * gd]).astype(BF16)
        pooled.append(yg)
    br_a = jnp.dot(jnp.concatenate(pooled, axis=1), wpo_ref[...], preferred_element_type=F32)
    new_hist = ext[tm + POOL_HEAD - HIST_ROWS:tm + POOL_HEAD, :]
    ext[POOL_HEAD - HIST_ROWS:POOL_HEAD, :] = new_hist

    o_q = POOL_WIDTH
    q = jnp.dot(h, win_ref[:, o_q:o_q + RET_WIDTH], preferred_element_type=F32)
    k = jnp.dot(h, win_ref[:, o_q + RET_WIDTH:o_q + 2 * RET_WIDTH], preferred_element_type=F32)
    v = jnp.dot(h, win_ref[:, o_q + 2 * RET_WIDTH:o_q + 3 * RET_WIDTH], preferred_element_type=F32)
    cos = cos_ref[r0:r0 + tm, :]
    sin = sin_ref[r0:r0 + tm, :]
    for hh in range(RET_HEADS):
        sl = slice(hh * RET_DK, (hh + 1) * RET_DK)
        qh = q[:, sl]
        kh = k[:, sl]
        qs[r0:r0 + tm, sl] = (qh * cos + pltpu.roll(qh, RET_DK // 2, 1) * sin).astype(BF16)
        ks[r0:r0 + tm, sl] = (kh * cos + pltpu.roll(kh, RET_DK // 2, 1) * sin) * (RET_DK ** -0.5)
    vs[r0:r0 + tm, :] = v.astype(BF16)
    for c in range(tm // chunk):
        rows = slice(r0 + c * chunk, r0 + (c + 1) * chunk)
        for hh in range(RET_HEADS):
            sl = slice(hh * RET_DK, (hh + 1) * RET_DK)
            qc = qs[rows, sl]
            kf = ks[rows, sl]
            kc = kf.astype(BF16)
            vc = vs[rows, sl]
            sc = lax.dot_general(qc, kc, (((1,), (1,)), ((), ())),
                                 preferred_element_type=F32) * dmask_ref[hh]
            inner = jnp.dot(sc.astype(BF16), vc, preferred_element_type=F32)
            s_h = s_acc[hh]
            cross = jnp.dot(qc, s_h.astype(BF16), preferred_element_type=F32) * xi_ref[hh]
            ys[rows, sl] = inner + cross
            kz = (kf * zeta_ref[hh]).astype(BF16)
            s_acc[hh] = gblk_ref[hh] * s_h + lax.dot_general(
                kz, vc, (((0,), (0,)), ((), ())), preferred_element_type=F32)

    zg = jnp.dot(h, win_ref[:, o_q + 3 * RET_WIDTH:o_q + 4 * RET_WIDTH], preferred_element_type=F32)
    normed = []
    for hh in range(RET_HEADS):
        sl = slice(hh * RET_DK, (hh + 1) * RET_DK)
        y = ys[r0:r0 + tm, sl]
        mu = jnp.mean(y, axis=-1, keepdims=True)
        dlt = y - mu
        var = jnp.mean(dlt * dlt, axis=-1, keepdims=True)
        normed.append(dlt * lax.rsqrt(var + EPS))
    yn = jnp.concatenate(normed, axis=1) * rgn_ref[...]
    ret_y = (zg * _sigmoid(zg)) * yn
    br_b = jnp.dot(ret_y.astype(BF16), wro_ref[...], preferred_element_type=F32)

    o_g = o_q + 4 * RET_WIDTH
    gts = _sigmoid(jnp.dot(h, win_ref[:, o_g:o_g + 2 * d], preferred_element_type=F32))
    merged = gts[:, 0:d] * br_a + gts[:, d:2 * d] * br_b
    o = jnp.dot(merged.astype(BF16), wo_ref[...], preferred_element_type=F32)
    o_ref[r0:r0 + tm, :] = x + gate * o


def _retention_tables(chunk):
    lg = jnp.log1p(-jnp.exp2(-5.0 - jnp.arange(RET_HEADS, dtype=F32)))
    i = jnp.arange(chunk, dtype=F32)
    diff = i[:, None] - i[None, :]
    dmask = jnp.where(diff[None] >= 0,
                      jnp.exp(lg[:, None, None] * jnp.maximum(diff, 0.0)[None]), 0.0)
    xi = jnp.exp(lg[:, None] * (i[None, :] + 1.0))
    zeta = jnp.exp(lg[:, None] * (chunk - 1.0 - i[None, :]))
    gblk = jnp.exp(lg * chunk)
    bcast = lambda a: jnp.broadcast_to(a[:, :, None], (RET_HEADS, a.shape[1], RET_DK))
    return dmask, bcast(xi), bcast(zeta), bcast(gblk[:, None])


def _rope_tables(start, t):
    half = RET_DK // 2
    inv = ROPE_BASE ** (-jnp.arange(half, dtype=F32) / half)
    pos = start + jnp.arange(t)
    ang = pos.astype(F32)[:, None] * inv[None, :]
    cos = jnp.cos(ang)
    sin = jnp.sin(ang)
    return jnp.concatenate([cos, cos], axis=1), jnp.concatenate([-sin, sin], axis=1)


def _mixer_call(x, mod4, norm_g3, tables, weights, hist, state, *, layer, row0, tm, n_sub,
                chunk, start):
    bsz, t, d = x.shape
    has_state = state is not None
    cos, sin, dmask, xi, zeta, gblk = tables
    win, wpo, rgn, wro, wo = weights
    tt = n_sub * tm
    grid = (bsz, t // tt)
    const3 = lambda b, i: (0, 0, 0)
    lay2 = lambda b, i: (layer, 0, 0)
    in_specs = [
        pl.BlockSpec((None, tt, d), lambda b, i: (b, i, 0)),
        pl.BlockSpec((None, None, 1, 3 * d), lambda b, i: (layer, row0 + b, 0, 1)),
        _resident((None, 1, d), lambda b, i: (layer * 3 + 1, 0, 0)),
        pl.BlockSpec((tt, RET_DK), lambda b, i: (i, 0)),
        pl.BlockSpec((tt, RET_DK), lambda b, i: (i, 0)),
        _resident((RET_HEADS, chunk, chunk), const3),
        _resident((RET_HEADS, chunk, RET_DK), const3),
        _resident((RET_HEADS, chunk, RET_DK), const3),
        _resident((RET_HEADS, 1, RET_DK), const3),
        _resident((None, d, IN_WIDTH), lay2),
        _resident((None, POOL_WIDTH, d), lay2),
        _resident((None, 1, RET_WIDTH), lay2),
        _resident((None, RET_WIDTH, d), lay2),
        _resident((None, d, d), lay2),
    ]
    args = [x, mod4, norm_g3, cos, sin, dmask, xi, zeta, gblk, win, wpo, rgn, wro, wo]
    if has_state:
        in_specs += [
            pl.BlockSpec((None, HIST_ROWS, POOL_WIDTH), lambda b, i: (b, 0, 0)),
            pl.BlockSpec((None, RET_HEADS, RET_DK, RET_DK), lambda b, i: (b, 0, 0, 0)),
        ]
        args += [hist, state]
    out_shape = (
        jax.ShapeDtypeStruct(x.shape, F32),
        jax.ShapeDtypeStruct((bsz, HIST_ROWS, POOL_WIDTH), F32),
        jax.ShapeDtypeStruct((bsz, RET_HEADS, RET_DK, RET_DK), F32),
    )
    out_specs = (
        pl.BlockSpec((None, tt, d), lambda b, i: (b, i, 0)),
        pl.BlockSpec((None, HIST_ROWS, POOL_WIDTH), lambda b, i: (b, 0, 0)),
        pl.BlockSpec((None, RET_HEADS, RET_DK, RET_DK), lambda b, i: (b, 0, 0, 0)),
    )
    scratch = [
        pltpu.VMEM((RET_HEADS, RET_DK, RET_DK), F32),
        pltpu.VMEM((POOL_HEAD + tm, POOL_WIDTH), F32),
        pltpu.VMEM((POOL_HEAD + tm, 3 * POOL_GD), F32),
        pltpu.VMEM((POOL_HEAD + tm, 2 * POOL_GD), F32),
        pltpu.VMEM((tt, RET_WIDTH), BF16),
        pltpu.VMEM((tt, RET_WIDTH), F32),
        pltpu.VMEM((tt, RET_WIDTH), BF16),
        pltpu.VMEM((tt, RET_WIDTH), F32),
    ]
    return pl.pallas_call(
        functools.partial(_mixer_kernel, chunk=chunk, start=start, has_state=has_state,
                          n_sub=n_sub),
        grid=grid,
        in_specs=in_specs,
        out_specs=out_specs,
        out_shape=out_shape,
        scratch_shapes=scratch,
        compiler_params=pltpu.CompilerParams(
            dimension_semantics=("arbitrary", "arbitrary"),
            vmem_limit_bytes=VMEM_LIMIT_BYTES),
        name="mixer_state" if has_state else "mixer",
    )(*args)


def _trunk(x, mod4, norm_g3, ffn_w, mix_w, final_g, hists, states, *, row0, start,
           ffn_nb, ffn_tt, mix_tm, mix_sub, chunk):
    t = x.shape[1]
    cos, sin = _rope_tables(start, t)
    tables = (cos, sin) + _retention_tables(chunk)
    wup1, wdn1, wup2, wdn2 = ffn_w
    new_hists, new_states = [], []
    for l in range(DEPTH):
        x = _ffn_call(x, mod4, norm_g3, wup1, wdn1, final_g, layer=l, sub=0, row0=row0,
                      nb=ffn_nb, tt=ffn_tt, final_norm=False)
        x, nh, ns = _mixer_call(
            x, mod4, norm_g3, tables, mix_w,
            None if hists is None else hists[l], None if states is None else states[l],
            layer=l, row0=row0, tm=mix_tm, n_sub=mix_sub, chunk=chunk, start=start)
        new_hists.append(nh[:, HIST_ROWS - POOL_BUF:, :])
        new_states.append(ns)
        x = _ffn_call(x, mod4, norm_g3, wup2, wdn2, final_g, layer=l, sub=2, row0=row0,
                      nb=ffn_nb, tt=ffn_tt, final_norm=(l == DEPTH - 1))
    return x, jnp.stack(new_hists, axis=0), jnp.stack(new_states, axis=0)


def kernel(x_prompt, x_sample, c_prompt, c_sample, cache_pool, state_ret, w_ada, b_ada, norm_g,
           w_up1, w_down1, w_up2, w_down2, w_in, pool_map, pool_scale, w_pool_out, ret_gn,
           w_ret_out, w_o, final_g):
    d = D_MODEL
    n_prompt = x_prompt.shape[0]
    n_sample = x_sample.shape[0]
    c = jnp.concatenate([c_prompt, c_sample], axis=0)
    mod = _mod_call(c, w_ada, b_ada)
    mod4 = mod.reshape(DEPTH, n_prompt + n_sample, 1, N_MOD * d)
    norm_g3 = norm_g.reshape(DEPTH * 3, 1, d)

    ffn_w = (w_up1.astype(BF16), w_down1.astype(BF16), w_up2.astype(BF16), w_down2.astype(BF16))
    mix_w = (w_in.astype(BF16), _pool_fold_call(pool_map, pool_scale, w_pool_out),
             ret_gn.reshape(DEPTH, 1, RET_WIDTH),
             w_ret_out.astype(BF16), w_o.astype(BF16))

    y_prompt, pool_prompt, ret_prompt = _trunk(
        x_prompt, mod4, norm_g3, ffn_w, mix_w, final_g, None, None, row0=0, start=0,
        ffn_nb=1, ffn_tt=FFN_SUBTILES * TOKEN_TILE, mix_tm=TOKEN_TILE, mix_sub=MIXER_SUBTILES,
        chunk=RET_CHUNK)

    t_s = x_sample.shape[1]
    hists = jnp.pad(cache_pool, ((0, 0), (0, 0), (HIST_ROWS - POOL_BUF, 0), (0, 0)))
    y_sample, pool_sample, ret_sample = _trunk(
        x_sample, mod4, norm_g3, ffn_w, mix_w, final_g, hists, state_ret, row0=n_prompt,
        start=PAST_LEN, ffn_nb=n_sample, ffn_tt=t_s, mix_tm=t_s, mix_sub=1, chunk=t_s)
    return (y_prompt, y_sample, pool_prompt, ret_prompt, pool_sample, ret_sample)
```
